```python
import math
import jax, jax.numpy as jnp
from jax import lax
import numpy as np

D_MODEL = 1024
BATCH = 2
SEQ = 8192
DEPTH = 1
DEC_BATCH = 128
DEC_SEQ = 4
PAST_LEN = 16384
PAGE_SIZE = 128

HEAD_DIM = 64
MIX_WIDTH = D_MODEL
ATT_WIDTH = MIX_WIDTH // 2
CONV_DIM = MIX_WIDTH - ATT_WIDTH
N_Q_HEADS = ATT_WIDTH // HEAD_DIM
N_KV_HEADS = 2
GQA_GROUP = N_Q_HEADS // N_KV_HEADS
CONV_GROUPS = CONV_DIM // HEAD_DIM
CONV_WIDTH = 3
WINDOW = 128
BLOCK = 128
N_BUCKETS = 32
MAX_DISTANCE = 128
D_FF = 2816
D_PLE = 256
RMS_EPS = 1e-6
IN_COLS = ATT_WIDTH + 2 * N_KV_HEADS * HEAD_DIM + 3 * CONV_DIM
NEG_INF = -1e30

kernel_name = "hybrid_swa_shortconv_macaron_decoder_step"


def _rms(x, g):
    x32 = x.astype(jnp.float32)
    y = x32 * lax.rsqrt(jnp.mean(x32 * x32, axis=-1, keepdims=True) + RMS_EPS)
    return y.astype(x.dtype) * g


def _swiglu(h, w_gate, w_up, w_down):
    return (jax.nn.silu(h @ w_gate) * (h @ w_up)) @ w_down


def _rel_bucket(dist):
    max_exact = N_BUCKETS // 2
    d = jnp.maximum(dist, 0)
    df = jnp.maximum(d, 1).astype(jnp.float32)
    large = max_exact + (jnp.log(df / max_exact) / math.log(MAX_DISTANCE / max_exact)
                         * (N_BUCKETS - max_exact)).astype(jnp.int32)
    large = jnp.minimum(large, N_BUCKETS - 1)
    return jnp.where(d < max_exact, d, large)


def _rel_bias(table, dist):
    b = table[_rel_bucket(dist)]
    qn, kn = dist.shape
    return jnp.transpose(b, (2, 0, 1)).reshape(N_KV_HEADS, GQA_GROUP, qn, kn)


def _sink_attention(q, k, v, bias, mask, sinks):
    s = jnp.einsum('...qhgd,...khd->...hgqk', q, k).astype(jnp.float32) * (HEAD_DIM ** -0.5)
    s = jnp.where(mask, s + bias.astype(jnp.float32), NEG_INF)
    sink = sinks.astype(jnp.float32)[..., None, None]
    m = jnp.maximum(jnp.max(s, axis=-1, keepdims=True), sink)
    e = jnp.exp(s - m)
    w = e / (jnp.sum(e, axis=-1, keepdims=True) + jnp.exp(sink - m))
    return jnp.einsum('...hgqk,...khd->...qhgd', w.astype(v.dtype), v)


def _attn_prompt(q, k, v, table, sinks):
    B, T = q.shape[:2]
    nb = T // BLOCK
    qb = q.reshape(B, nb, BLOCK, N_KV_HEADS, GQA_GROUP, HEAD_DIM)
    kb = k.reshape(B, nb, BLOCK, N_KV_HEADS, HEAD_DIM)
    vb = v.reshape(B, nb, BLOCK, N_KV_HEADS, HEAD_DIM)
    pad = jnp.zeros_like(kb[:, :1])
    kc = jnp.concatenate([jnp.concatenate([pad, kb[:, :-1]], axis=1), kb], axis=2)
    vc = jnp.concatenate([jnp.concatenate([pad, vb[:, :-1]], axis=1), vb], axis=2)
    qi = jnp.arange(BLOCK)[:, None]
    kj = jnp.arange(2 * BLOCK)[None, :]
    dist = qi + BLOCK - kj
    valid_key = (jnp.arange(nb)[:, None, None] * BLOCK - BLOCK + kj[None]) >= 0
    mask = (dist >= 0)[None] & (dist < WINDOW)[None] & valid_key
    bias = _rel_bias(table, dist)
    o = _sink_attention(qb, kc, vc, bias, mask[None, :, None, None], sinks)
    return o.reshape(B, T, ATT_WIDTH)


def _attn_sample(q, k_new, v_new, k_cache, v_cache, table, sinks):
    Bd, S = q.shape[:2]
    W = k_cache.shape[1]
    kc = jnp.concatenate([k_cache, k_new], axis=1)
    vc = jnp.concatenate([v_cache, v_new], axis=1)
    dist = jnp.arange(S)[:, None] + W - jnp.arange(W + S)[None, :]
    mask = (dist >= 0) & (dist < WINDOW)
    bias = _rel_bias(table, dist)
    qg = q.reshape(Bd, S, N_KV_HEADS, GQA_GROUP, HEAD_DIM)
    o = _sink_attention(qg, kc, vc, bias, mask, sinks)
    return o.reshape(Bd, S, ATT_WIDTH), kc[:, -W:], vc[:, -W:]


def _short_conv(u, buf, w_conv):
    T = u.shape[1]
    full = jnp.concatenate([buf, u], axis=1)
    y = w_conv[0] * full[:, 0:T]
    for j in range(1, CONV_WIDTH):
        y = y + w_conv[j] * full[:, j:j + T]
    return y, full[:, -(CONV_WIDTH - 1):]


def _layer(x, pe, lw, rel_bias, k_cache=None, v_cache=None, conv_buf=None):
    (g1, w1g, w1u, w1d, gm, w_in, qn, kn, sinks, w_conv, w_out,
     g2, w2g, w2u, w2d, gp, wpg, wpp) = lw
    B, T, _ = x.shape
    x = x + 0.5 * _swiglu(_rms(x, g1), w1g, w1u, w1d)
    z = _rms(x, gm) @ w_in
    o1 = ATT_WIDTH
    o2 = o1 + N_KV_HEADS * HEAD_DIM
    o3 = o2 + N_KV_HEADS * HEAD_DIM
    o4 = o3 + CONV_DIM
    o5 = o4 + CONV_DIM
    q = _rms(z[..., :o1].reshape(B, T, N_Q_HEADS, HEAD_DIM), qn)
    k = _rms(z[..., o1:o2].reshape(B, T, N_KV_HEADS, HEAD_DIM), kn)
    v = z[..., o2:o3].reshape(B, T, N_KV_HEADS, HEAD_DIM)
    gate_b = z[..., o3:o4]
    gate_c = z[..., o4:o5]
    h = z[..., o5:]
    sinks_g = sinks.reshape(N_KV_HEADS, GQA_GROUP)
    if k_cache is None:
        att = _attn_prompt(q, k, v, rel_bias, sinks_g)
        w = min(WINDOW, T)
        k_state, v_state = k[:, T - w:], v[:, T - w:]
        conv_buf = jnp.zeros((B, CONV_WIDTH - 1, CONV_DIM), x.dtype)
    else:
        att, k_state, v_state = _attn_sample(q, k, v, k_cache, v_cache, rel_bias, sinks_g)
    cv, conv_state = _short_conv(gate_c * h, conv_buf, w_conv)
    x = x + jnp.concatenate([att, gate_b * cv], axis=-1) @ w_out
    x = x + 0.5 * _swiglu(_rms(x, g2), w2g, w2u, w2d)
    x = x + jax.nn.sigmoid(_rms(x, gp) @ wpg) * (pe @ wpp)
    return x, k_state, v_state, conv_state


def setup_inputs(seed: int = 0) -> dict:
    key = jax.random.key(seed)
    ks = jax.random.split(key, 32)
    f32 = jnp.float32

    def nrm(k, shape, scale):
        return jax.random.normal(k, shape, f32) * scale

    def gain(k, shape):
        return 1.0 + 0.05 * jax.random.normal(k, shape, f32)

    win = min(WINDOW, PAST_LEN)
    D = D_MODEL
    return {
        "x_prompt": nrm(ks[0], (BATCH, SEQ, D), 1.0),
        "x_sample": nrm(ks[1], (DEC_BATCH, DEC_SEQ, D), 1.0),
        "p_prompt": nrm(ks[2], (DEPTH, BATCH, SEQ, D_PLE), 1.0),
        "p_sample": nrm(ks[3], (DEPTH, DEC_BATCH, DEC_SEQ, D_PLE), 1.0),
        "cache_k": nrm(ks[4], (DEPTH, DEC_BATCH, win, N_KV_HEADS, HEAD_DIM), 1.0),
        "cache_v": nrm(ks[5], (DEPTH, DEC_BATCH, win, N_KV_HEADS, HEAD_DIM), 1.0),
        "state_conv": nrm(ks[6], (DEPTH, DEC_BATCH, CONV_WIDTH - 1, CONV_DIM), 1.0),
        "rel_bias": nrm(ks[7], (N_BUCKETS, N_Q_HEADS), 0.5),
        "g_ffn1": gain(ks[8], (DEPTH, D)),
        "w1_gate": nrm(ks[9], (DEPTH, D, D_FF), D ** -0.5),
        "w1_up": nrm(ks[10], (DEPTH, D, D_FF), D ** -0.5),
        "w1_down": nrm(ks[11], (DEPTH, D_FF, D), D_FF ** -0.5),
        "g_mix": gain(ks[12], (DEPTH, D)),
        "w_in": nrm(ks[13], (DEPTH, D, IN_COLS), D ** -0.5),
        "q_norm": gain(ks[14], (DEPTH, HEAD_DIM)),
        "k_norm": gain(ks[15], (DEPTH, HEAD_DIM)),
        "sinks": nrm(ks[16], (DEPTH, N_Q_HEADS), 1.0),
        "w_conv": nrm(ks[17], (DEPTH, CONV_WIDTH, CONV_DIM), CONV_WIDTH ** -0.5),
        "w_out": nrm(ks[18], (DEPTH, MIX_WIDTH, D), MIX_WIDTH ** -0.5),
        "g_ffn2": gain(ks[19], (DEPTH, D)),
        "w2_gate": nrm(ks[20], (DEPTH, D, D_FF), D ** -0.5),
        "w2_up": nrm(ks[21], (DEPTH, D, D_FF), D ** -0.5),
        "w2_down": nrm(ks[22], (DEPTH, D_FF, D), D_FF ** -0.5),
        "g_ple": gain(ks[23], (DEPTH, D)),
        "w_ple_gate": nrm(ks[24], (DEPTH, D, D), D ** -0.5),
        "w_ple_proj": nrm(ks[25], (DEPTH, D_PLE, D), D_PLE ** -0.5),
    }


def reference(x_prompt, x_sample, p_prompt, p_sample, cache_k, cache_v, state_conv, rel_bias,
              g_ffn1, w1_gate, w1_up, w1_down, g_mix, w_in, q_norm, k_norm, sinks, w_conv, w_out,
              g_ffn2, w2_gate, w2_up, w2_down, g_ple, w_ple_gate, w_ple_proj):
    xp, xs = x_prompt, x_sample
    kp_l, vp_l, cp_l, ks_l, vs_l, cs_l = [], [], [], [], [], []
    for i in range(DEPTH):
        lw = (g_ffn1[i], w1_gate[i], w1_up[i], w1_down[i], g_mix[i], w_in[i], q_norm[i], k_norm[i],
              sinks[i], w_conv[i], w_out[i], g_ffn2[i], w2_gate[i], w2_up[i], w2_down[i],
              g_ple[i], w_ple_gate[i], w_ple_proj[i])
        xp, kp, vp, cp = _layer(xp, p_prompt[i], lw, rel_bias)
        xs, ksn, vsn, csn = _layer(xs, p_sample[i], lw, rel_bias,
                                   k_cache=cache_k[i], v_cache=cache_v[i], conv_buf=state_conv[i])
        kp_l.append(kp); vp_l.append(vp); cp_l.append(cp)
        ks_l.append(ksn); vs_l.append(vsn); cs_l.append(csn)
    k_win_prompt = jnp.stack(kp_l)
    v_win_prompt = jnp.stack(vp_l)
    conv_prompt = jnp.stack(cp_l)
    k_win_sample = jnp.stack(ks_l)
    v_win_sample = jnp.stack(vs_l)
    conv_sample = jnp.stack(cs_l)
    return (xp, xs, k_win_prompt, v_win_prompt, conv_prompt, k_win_sample, v_win_sample, conv_sample)
```

```python
import functools
import math

import jax
import jax.numpy as jnp
import numpy as np
from jax import lax
from jax.experimental import pallas as pl
from jax.experimental.pallas import tpu as pltpu

D_MODEL = 1024
HEAD_DIM = 64
N_Q_HEADS = 8
N_KV_HEADS = 2
GQA_GROUP = 4
ATT_WIDTH = 512
CONV_DIM = 512
CONV_WIDTH = 3
WINDOW = 128
BLOCK = 128
N_BUCKETS = 32
MAX_DISTANCE = 128
D_FF = 2816
D_PLE = 256
RMS_EPS = 1e-6
NEG_INF = -1e30
KV_COLS = N_KV_HEADS * HEAD_DIM
IN_COLS = ATT_WIDTH + 2 * KV_COLS + 3 * CONV_DIM

V7X_LANES = 128
V7X_VMEM_LIMIT_BYTES = 60 * 1024 * 1024

TM = 256
SEQ_TILE = 64
ATT_SEQS = 32
KEY_PAD = 256

BF16 = jnp.bfloat16
F32 = jnp.float32


def _rms(x, g):
    ms = jnp.mean(x * x, axis=-1, keepdims=True)
    return x * lax.rsqrt(ms + RMS_EPS) * g


def _head_rms(z):
    m, c = z.shape
    lo = lax.broadcasted_iota(jnp.int32, (m, V7X_LANES), 1) < HEAD_DIM
    outs = []
    for s in range(c // V7X_LANES):
        zc = z[:, s * V7X_LANES:(s + 1) * V7X_LANES]
        sq = zc * zc
        ms_lo = jnp.sum(jnp.where(lo, sq, 0.0), axis=-1, keepdims=True) / HEAD_DIM
        ms_hi = jnp.sum(jnp.where(lo, 0.0, sq), axis=-1, keepdims=True) / HEAD_DIM
        inv = jnp.where(lo, lax.rsqrt(ms_lo + RMS_EPS), lax.rsqrt(ms_hi + RMS_EPS))
        outs.append(zc * inv)
    return jnp.concatenate(outs, axis=1)


def _swiglu(h, wg_ref, wu_ref, wd_ref):
    g = jnp.dot(h, wg_ref[...], preferred_element_type=F32)
    u = jnp.dot(h, wu_ref[...], preferred_element_type=F32)
    a = (jax.nn.silu(g) * u).astype(BF16)
    return jnp.dot(a, wd_ref[...], preferred_element_type=F32)


def _tail(x1, mix, p, wout_ref, g2_ref, w2g_ref, w2u_ref, w2d_ref, gp_ref, wpg_ref, wpp_ref):
    x2 = x1 + jnp.dot(mix.astype(BF16), wout_ref[...], preferred_element_type=F32)
    x3 = x2 + 0.5 * _swiglu(_rms(x2, g2_ref[...]).astype(BF16), w2g_ref, w2u_ref, w2d_ref)
    gate = jax.nn.sigmoid(jnp.dot(_rms(x3, gp_ref[...]).astype(BF16), wpg_ref[...],
                                  preferred_element_type=F32))
    return x3 + gate * jnp.dot(p.astype(BF16), wpp_ref[...], preferred_element_type=F32)


def _rel_bucket_np(dist):
    max_exact = N_BUCKETS // 2
    d = np.maximum(dist, 0)
    df = np.maximum(d, 1).astype(np.float32)
    large = max_exact + (np.log(df / np.float32(max_exact)) / np.float32(math.log(MAX_DISTANCE / max_exact))
                         * np.float32(N_BUCKETS - max_exact)).astype(np.int32)
    large = np.minimum(large, N_BUCKETS - 1)
    return np.where(d < max_exact, d, large).astype(np.int32)


def _bucket_tiles(dec_seq):
    qi = np.arange(BLOCK)[:, None]
    kj = np.arange(2 * BLOCK)[None, :]
    dist = qi + BLOCK - kj
    ok = (dist >= 0) & (dist < WINDOW)
    mid = np.where(ok, _rel_bucket_np(dist), -1)
    first = np.where(ok & (kj >= BLOCK), _rel_bucket_np(dist), -1)
    idx_p = np.stack([first, mid]).astype(np.int32)
    s = np.arange(8)[:, None]
    j = np.arange(KEY_PAD)[None, :]
    dist_s = s + WINDOW - j
    ok_s = (dist_s >= 0) & (dist_s < WINDOW) & (j < WINDOW + dec_seq) & (s < dec_seq)
    idx_s = np.where(ok_s, _rel_bucket_np(dist_s), -1).astype(np.int32)
    return idx_p, idx_s


def _bias_kernel(tab_ref, idxp_ref, idxs_ref, outp_ref, outs_ref):
    def lookup(idx, n):
        acc = jnp.full(idx.shape, NEG_INF, F32)
        for b in range(N_BUCKETS):
            acc = jnp.where(idx == b, tab_ref[b, n], acc)
        return acc

    for v in range(2):
        idx = idxp_ref[v]
        for n in range(N_Q_HEADS):
            outp_ref[v, n] = lookup(idx, n)
    idx = idxs_ref[...]
    for n in range(N_Q_HEADS):
        outs_ref[n] = lookup(idx, n)


def _bias_tiles(rel_bias, dec_seq):
    idx_p, idx_s = _bucket_tiles(dec_seq)
    return pl.pallas_call(
        _bias_kernel,
        out_shape=(jax.ShapeDtypeStruct((2, N_Q_HEADS, BLOCK, 2 * BLOCK), F32),
                   jax.ShapeDtypeStruct((N_Q_HEADS, 8, KEY_PAD), F32)),
        in_specs=[pl.BlockSpec(memory_space=pltpu.SMEM),
                  pl.BlockSpec(memory_space=pltpu.VMEM),
                  pl.BlockSpec(memory_space=pltpu.VMEM)],
        out_specs=(pl.BlockSpec(memory_space=pltpu.VMEM), pl.BlockSpec(memory_space=pltpu.VMEM)),
        name="rel_bias_tiles",
    )(rel_bias, jnp.asarray(idx_p), jnp.asarray(idx_s))


def _kernel_a(x_ref, g1_ref, w1g_ref, w1u_ref, w1d_ref, gm_ref, win_ref, qg_ref, kg_ref,
              x1_ref, q_ref, kv_ref, gb_ref, u_ref):
    x = x_ref[...]
    x1 = x + 0.5 * _swiglu(_rms(x, g1_ref[...]).astype(BF16), w1g_ref, w1u_ref, w1d_ref)
    x1_ref[...] = x1
    z = jnp.dot(_rms(x1, gm_ref[...]).astype(BF16), win_ref[...], preferred_element_type=F32)
    o1 = ATT_WIDTH
    o2 = o1 + KV_COLS
    o3 = o2 + KV_COLS
    o4 = o3 + CONV_DIM
    o5 = o4 + CONV_DIM
    q_ref[...] = (_head_rms(z[:, :o1]) * qg_ref[...]).astype(BF16)
    kv_ref[:, :KV_COLS] = _head_rms(z[:, o1:o2]) * kg_ref[...]
    kv_ref[:, KV_COLS:] = z[:, o2:o3]
    gb_ref[...] = z[:, o3:o4]
    u_ref[...] = z[:, o4:o5] * z[:, o5:]


def _resident(shape):
    return pl.BlockSpec(shape, lambda i: (0,) * len(shape), pipeline_mode=pl.Buffered(1))


def _rows(cols):
    return pl.BlockSpec((TM, cols), lambda i: (i, 0))


def _call_a(x, w):
    n = x.shape[0]
    return pl.pallas_call(
        _kernel_a,
        grid=(n // TM,),
        out_shape=(jax.ShapeDtypeStruct((n, D_MODEL), F32),
                   jax.ShapeDtypeStruct((n, ATT_WIDTH), BF16),
                   jax.ShapeDtypeStruct((n, 2 * KV_COLS), F32),
                   jax.ShapeDtypeStruct((n, CONV_DIM), F32),
                   jax.ShapeDtypeStruct((n, CONV_DIM), F32)),
        in_specs=[_rows(D_MODEL), _resident((1, D_MODEL)),
                  _resident((D_MODEL, D_FF)), _resident((D_MODEL, D_FF)), _resident((D_FF, D_MODEL)),
                  _resident((1, D_MODEL)), _resident((D_MODEL, IN_COLS)),
                  _resident((1, ATT_WIDTH)), _resident((1, KV_COLS))],
        out_specs=(_rows(D_MODEL), _rows(ATT_WIDTH), _rows(2 * KV_COLS), _rows(CONV_DIM), _rows(CONV_DIM)),
        compiler_params=pltpu.CompilerParams(dimension_semantics=("arbitrary",),
                                             vmem_limit_bytes=V7X_VMEM_LIMIT_BYTES),
        name="ffn1_inproj",
    )(x, w["g1"], w["w1g"], w["w1u"], w["w1d"], w["gm"], w["win"], w["qg"], w["kg"])


def _kernel_b_prompt(tiles_per_seq, sink_ref, x1_ref, q_ref, kv_ref, kvp_ref, gb_ref, u_ref, up_ref, p_ref,
                     bias_ref, wc_ref, wout_ref, g2_ref, w2g_ref, w2u_ref, w2d_ref, gp_ref, wpg_ref, wpp_ref,
                     y_ref, ufull_ref):
    i = pl.program_id(0)
    first = (i % tiles_per_seq) == 0
    lane = lax.broadcasted_iota(jnp.int32, (1, V7X_LANES), 1)
    head_lanes = [lane < HEAD_DIM, lane >= HEAD_DIM]
    grow = lax.broadcasted_iota(jnp.int32, (GQA_GROUP * BLOCK, 1), 0) // BLOCK
    sink_cols = []
    for h in range(N_KV_HEADS):
        col = jnp.zeros((GQA_GROUP * BLOCK, 1), F32)
        for g in range(GQA_GROUP):
            col = jnp.where(grow == g, sink_ref[h * GQA_GROUP + g], col)
        sink_cols.append(col)

    k_all = jnp.concatenate([kvp_ref[:, :KV_COLS], kv_ref[:, :KV_COLS]], axis=0).astype(BF16)
    v_all = jnp.concatenate([kvp_ref[:, KV_COLS:], kv_ref[:, KV_COLS:]], axis=0).astype(BF16)
    att_blocks = []
    for j in range(TM // BLOCK):
        kc = k_all[j * BLOCK:(j + 2) * BLOCK]
        vc = v_all[j * BLOCK:(j + 2) * BLOCK]
        qb = q_ref[j * BLOCK:(j + 1) * BLOCK, :]
        qs = jnp.concatenate([qb[:, g * V7X_LANES:(g + 1) * V7X_LANES] for g in range(GQA_GROUP)], axis=0)
        variant = jnp.where(first, 0, 1) if j == 0 else 1
        o_acc = None
        for h in range(N_KV_HEADS):
            kh = jnp.where(head_lanes[h], kc, jnp.zeros_like(kc))
            vh = jnp.where(head_lanes[h], vc, jnp.zeros_like(vc))
            s = lax.dot_general(qs, kh, (((1,), (1,)), ((), ())), preferred_element_type=F32)
            s = s + bias_ref[variant, h]
            mx = jnp.maximum(jnp.max(s, axis=-1, keepdims=True), sink_cols[h])
            e = jnp.exp(s - mx)
            den = jnp.sum(e, axis=-1, keepdims=True) + jnp.exp(sink_cols[h] - mx)
            o = jnp.dot(e.astype(BF16), vh, preferred_element_type=F32) / den
            o_acc = o if o_acc is None else o_acc + o
        att_blocks.append(jnp.concatenate(
            [o_acc[g * BLOCK:(g + 1) * BLOCK] for g in range(GQA_GROUP)], axis=1))
    att = jnp.concatenate(att_blocks, axis=0)

    pad = ufull_ref.shape[0] - TM
    ufull_ref[:pad, :] = jnp.where(first, 0.0, up_ref[...])
    ufull_ref[pad:, :] = u_ref[...]
    cv = wc_ref[CONV_WIDTH - 1:CONV_WIDTH, :] * u_ref[...]
    for jj in range(CONV_WIDTH - 1):
        back = CONV_WIDTH - 1 - jj
        cv = cv + wc_ref[jj:jj + 1, :] * ufull_ref[pl.ds(pad - back, TM), :]
    mix = jnp.concatenate([att, gb_ref[...] * cv], axis=1)
    y_ref[...] = _tail(x1_ref[...], mix, p_ref[...], wout_ref, g2_ref, w2g_ref, w2u_ref, w2d_ref,
                       gp_ref, wpg_ref, wpp_ref)


def _tail_specs():
    return [_resident((D_MODEL, D_MODEL)), _resident((1, D_MODEL)),
            _resident((D_MODEL, D_FF)), _resident((D_MODEL, D_FF)), _resident((D_FF, D_MODEL)),
            _resident((1, D_MODEL)), _resident((D_MODEL, D_MODEL)), _resident((D_PLE, D_MODEL))]


def _tail_args(w):
    return (w["wout"], w["g2"], w["w2g"], w["w2u"], w["w2d"], w["gp"], w["wpg"], w["wpp"])


def _call_b_prompt(x1, q, kv, gb, u, p, bias_p, w, seq):
    n = x1.shape[0]
    blocks_per_tile = TM // BLOCK
    pad = 8
    return pl.pallas_call(
        functools.partial(_kernel_b_prompt, seq // TM),
        grid=(n // TM,),
        out_shape=jax.ShapeDtypeStruct((n, D_MODEL), F32),
        in_specs=[pl.BlockSpec(memory_space=pltpu.SMEM),
                  _rows(D_MODEL), _rows(ATT_WIDTH), _rows(2 * KV_COLS),
                  pl.BlockSpec((BLOCK, 2 * KV_COLS), lambda i: (jnp.maximum(i * blocks_per_tile - 1, 0), 0)),
                  _rows(CONV_DIM), _rows(CONV_DIM),
                  pl.BlockSpec((pad, CONV_DIM), lambda i: (jnp.maximum(i * (TM // pad) - 1, 0), 0)),
                  _rows(D_PLE),
                  _resident((2, N_KV_HEADS, GQA_GROUP * BLOCK, 2 * BLOCK)),
                  _resident((CONV_WIDTH, CONV_DIM))] + _tail_specs(),
        out_specs=_rows(D_MODEL),
        scratch_shapes=[pltpu.VMEM((TM + pad, CONV_DIM), F32)],
        compiler_params=pltpu.CompilerParams(dimension_semantics=("arbitrary",),
                                             vmem_limit_bytes=V7X_VMEM_LIMIT_BYTES),
        name="attn_conv_ffn2_prompt",
    )(w["sinks"], x1, q, kv, kv, gb, u, u, p, bias_p, w["wconv"], *_tail_args(w))


def _kernel_sample_attn(dec_seq, q_ref, ck_ref, cv_ref, kn_ref, vn_ref, bias_ref, sink_ref,
                        att_ref, kw_ref, vw_ref, kcat_ref, vcat_ref):
    lane = lax.broadcasted_iota(jnp.int32, (1, 1, V7X_LANES), 2)
    lo = lane < HEAD_DIM
    for cache_ref, new_ref, cat_ref, win_ref in ((ck_ref, kn_ref, kcat_ref, kw_ref),
                                                 (cv_ref, vn_ref, vcat_ref, vw_ref)):
        cat_ref[:, :WINDOW, :] = cache_ref[...]
        cat_ref[:, WINDOW:WINDOW + dec_seq, :] = new_ref[...]
        cat_ref[:, WINDOW + dec_seq:, :] = jnp.zeros(
            (cat_ref.shape[0], KEY_PAD - WINDOW - dec_seq, V7X_LANES), F32)
        win_ref[...] = cat_ref[:, pl.ds(dec_seq, WINDOW), :]
    q = q_ref[...]
    zero = jnp.zeros_like(q)
    q2 = jnp.concatenate([jnp.where(lo, q, zero), jnp.where(lo, zero, q)], axis=1)
    s = jnp.einsum("bqd,bkd->bqk", q2, kcat_ref[...].astype(BF16), preferred_element_type=F32)
    s = s + bias_ref[...][None]
    sink = sink_ref[:, 0:1][None]
    mx = jnp.maximum(jnp.max(s, axis=-1, keepdims=True), sink)
    e = jnp.exp(s - mx)
    den = jnp.sum(e, axis=-1, keepdims=True) + jnp.exp(sink - mx)
    o = jnp.einsum("bqk,bkd->bqd", e.astype(BF16), vcat_ref[...].astype(BF16),
                   preferred_element_type=F32) / den
    half = o.shape[1] // 2
    att_ref[...] = jnp.where(lo, o[:, :half], o[:, half:])


def _call_sample_attn(qw, cache_k, cache_v, k_new, v_new, bias_s, sink_rows, dec_seq):
    nb = qw.shape[0]
    rows = GQA_GROUP * dec_seq
    seq3 = lambda r: pl.BlockSpec((ATT_SEQS, r, V7X_LANES), lambda i: (i, 0, 0))
    return pl.pallas_call(
        functools.partial(_kernel_sample_attn, dec_seq),
        grid=(nb // ATT_SEQS,),
        out_shape=(jax.ShapeDtypeStruct((nb, rows, V7X_LANES), F32),
                   jax.ShapeDtypeStruct((nb, WINDOW, KV_COLS), F32),
                   jax.ShapeDtypeStruct((nb, WINDOW, KV_COLS), F32)),
        in_specs=[seq3(rows), seq3(WINDOW), seq3(WINDOW), seq3(dec_seq), seq3(dec_seq),
                  pl.BlockSpec((N_KV_HEADS * rows, KEY_PAD), lambda i: (0, 0)),
                  pl.BlockSpec((N_KV_HEADS * rows, V7X_LANES), lambda i: (0, 0))],
        out_specs=(seq3(rows), seq3(WINDOW), seq3(WINDOW)),
        scratch_shapes=[pltpu.VMEM((ATT_SEQS, KEY_PAD, V7X_LANES), F32),
                        pltpu.VMEM((ATT_SEQS, KEY_PAD, V7X_LANES), F32)],
        compiler_params=pltpu.CompilerParams(dimension_semantics=("arbitrary",),
                                             vmem_limit_bytes=V7X_VMEM_LIMIT_BYTES),
        name="sample_window_attention",
    )(qw, cache_k, cache_v, k_new, v_new, bias_s, sink_rows)


def _kernel_b_sample(x1_ref, att_ref, gb_ref, u_ref, st_ref, p_ref, wc_ref,
                     wout_ref, g2_ref, w2g_ref, w2u_ref, w2d_ref, gp_ref, wpg_ref, wpp_ref, y_ref):
    ufull = jnp.concatenate([st_ref[...], u_ref[...]], axis=0)
    cv = wc_ref[0:1, :] * ufull[0:TM]
    for jj in range(1, CONV_WIDTH):
        cv = cv + wc_ref[jj:jj + 1, :] * ufull[jj * SEQ_TILE:jj * SEQ_TILE + TM]
    mix = jnp.concatenate([att_ref[...], gb_ref[...] * cv], axis=1)
    y_ref[...] = _tail(x1_ref[...], mix, p_ref[...], wout_ref, g2_ref, w2g_ref, w2u_ref, w2d_ref,
                       gp_ref, wpg_ref, wpp_ref)


def _call_b_sample(x1, att, gb, u, state, p, w):
    n = x1.shape[0]
    st_rows = (CONV_WIDTH - 1) * SEQ_TILE
    return pl.pallas_call(
        _kernel_b_sample,
        grid=(n // TM,),
        out_shape=jax.ShapeDtypeStruct((n, D_MODEL), F32),
        in_specs=[_rows(D_MODEL), _rows(ATT_WIDTH), _rows(CONV_DIM), _rows(CONV_DIM),
                  pl.BlockSpec((st_rows, CONV_DIM), lambda i: (i, 0)), _rows(D_PLE),
                  _resident((CONV_WIDTH, CONV_DIM))] + _tail_specs(),
        out_specs=_rows(D_MODEL),
        compiler_params=pltpu.CompilerParams(dimension_semantics=("arbitrary",),
                                             vmem_limit_bytes=V7X_VMEM_LIMIT_BYTES),
        name="conv_ffn2_sample",
    )(x1, att, gb, u, state, p, w["wconv"], *_tail_args(w))


def _q_perm():
    g, h, d = np.meshgrid(np.arange(GQA_GROUP), np.arange(N_KV_HEADS), np.arange(HEAD_DIM), indexing="ij")
    return ((h * GQA_GROUP + g) * HEAD_DIM + d).reshape(-1)


def _layer_weights(i, g_ffn1, w1_gate, w1_up, w1_down, g_mix, w_in, q_norm, k_norm, sinks, w_conv, w_out,
                   g_ffn2, w2_gate, w2_up, w2_down, g_ple, w_ple_gate, w_ple_proj):
    perm = _q_perm()
    win = w_in[i]
    wout = w_out[i]
    row = lambda v: v.reshape(1, -1)
    return dict(
        g1=row(g_ffn1[i]), w1g=w1_gate[i].astype(BF16), w1u=w1_up[i].astype(BF16), w1d=w1_down[i].astype(BF16),
        gm=row(g_mix[i]),
        win=jnp.concatenate([win[:, perm], win[:, ATT_WIDTH:]], axis=1).astype(BF16),
        qg=row(jnp.tile(q_norm[i], N_Q_HEADS) * (HEAD_DIM ** -0.5)),
        kg=row(jnp.tile(k_norm[i], N_KV_HEADS)),
        sinks=sinks[i], wconv=w_conv[i],
        wout=jnp.concatenate([wout[perm, :], wout[ATT_WIDTH:, :]], axis=0).astype(BF16),
        g2=row(g_ffn2[i]), w2g=w2_gate[i].astype(BF16), w2u=w2_up[i].astype(BF16), w2d=w2_down[i].astype(BF16),
        gp=row(g_ple[i]), wpg=w_ple_gate[i].astype(BF16), wpp=w_ple_proj[i].astype(BF16))


def _to_tiles(a, dec_seq):
    b, s, c = a.shape
    return a.reshape(b // SEQ_TILE, SEQ_TILE, s, c).transpose(0, 2, 1, 3).reshape(b * s, c)


def _from_tiles(a, b, s):
    c = a.shape[-1]
    return a.reshape(b // SEQ_TILE, s, SEQ_TILE, c).transpose(0, 2, 1, 3).reshape(b, s, c)


def kernel(x_prompt, x_sample, p_prompt, p_sample, cache_k, cache_v, state_conv, rel_bias, g_ffn1, w1_gate, w1_up, w1_down, g_mix, w_in, q_norm, k_norm, sinks, w_conv, w_out, g_ffn2, w2_gate, w2_up, w2_down, g_ple, w_ple_gate, w_ple_proj):
    depth = p_prompt.shape[0]
    batch, seq, _ = x_prompt.shape
    dec_batch, dec_seq, _ = x_sample.shape
    win = cache_k.shape[2]
    assert win == WINDOW and seq % TM == 0 and SEQ_TILE * dec_seq == TM and dec_batch % SEQ_TILE == 0

    bias_p, bias_s = _bias_tiles(rel_bias, dec_seq)
    bias_p = bias_p.reshape(2, N_KV_HEADS, GQA_GROUP * BLOCK, 2 * BLOCK)
    bias_s = bias_s[:, :dec_seq, :].reshape(N_Q_HEADS * dec_seq, KEY_PAD)

    xp = x_prompt.reshape(batch * seq, D_MODEL)
    xs = _to_tiles(x_sample, dec_seq)
    outs = [[] for _ in range(6)]
    for i in range(depth):
        w = _layer_weights(i, g_ffn1, w1_gate, w1_up, w1_down, g_mix, w_in, q_norm, k_norm, sinks, w_conv,
                           w_out, g_ffn2, w2_gate, w2_up, w2_down, g_ple, w_ple_gate, w_ple_proj)
        x1, q, kv, gb, u = _call_a(xp, w)
        xp = _call_b_prompt(x1, q, kv, gb, u, p_prompt[i].reshape(batch * seq, D_PLE), bias_p, w, seq)
        kv3 = kv.reshape(batch, seq, 2 * KV_COLS)[:, seq - WINDOW:]
        outs[0].append(kv3[..., :KV_COLS].reshape(batch, WINDOW, N_KV_HEADS, HEAD_DIM))
        outs[1].append(kv3[..., KV_COLS:].reshape(batch, WINDOW, N_KV_HEADS, HEAD_DIM))
        outs[2].append(u.reshape(batch, seq, CONV_DIM)[:, seq - (CONV_WIDTH - 1):])

        x1, q, kv, gb, u = _call_a(xs, w)
        tiles = dec_batch // SEQ_TILE
        per_seq = lambda a, c: a.reshape(tiles, dec_seq, SEQ_TILE, c).transpose(0, 2, 1, 3).reshape(
            dec_batch, dec_seq, c)
        qw = per_seq(q, ATT_WIDTH).reshape(dec_batch, dec_seq, GQA_GROUP, V7X_LANES)
        qw = qw.transpose(0, 2, 1, 3).reshape(dec_batch, GQA_GROUP * dec_seq, V7X_LANES)
        kv_new = per_seq(kv, 2 * KV_COLS)
        sink_rows = jnp.broadcast_to(jnp.repeat(sinks[i], dec_seq)[:, None], (N_Q_HEADS * dec_seq, V7X_LANES))
        att, k_win, v_win = _call_sample_attn(
            qw, cache_k[i].reshape(dec_batch, WINDOW, KV_COLS), cache_v[i].reshape(dec_batch, WINDOW, KV_COLS),
            kv_new[..., :KV_COLS], kv_new[..., KV_COLS:], bias_s, sink_rows, dec_seq)
        att = att.reshape(dec_batch, GQA_GROUP, dec_seq, V7X_LANES).transpose(0, 2, 1, 3)
        att = _to_tiles(att.reshape(dec_batch, dec_seq, ATT_WIDTH), dec_seq)
        state = _to_tiles(state_conv[i], dec_seq)
        xs = _call_b_sample(x1, att, gb, u, state, _to_tiles(p_sample[i], dec_seq), w)
        outs[3].append(k_win.reshape(dec_batch, WINDOW, N_KV_HEADS, HEAD_DIM))
        outs[4].append(v_win.reshape(dec_batch, WINDOW, N_KV_HEADS, HEAD_DIM))
        outs[5].append(_from_tiles(u, dec_batch, dec_seq)[:, dec_seq - (CONV_WIDTH - 1):])

    y_prompt = xp.reshape(batch, seq, D_MODEL)
    y_sample = _from_tiles(xs, dec_batch, dec_seq)
    return (y_prompt, y_sample) + tuple(jnp.stack(o) for o in outs)
```
